```python
import jax, jax.numpy as jnp
from jax import lax
import numpy as np

D_MODEL = 1024
BATCH = 16
SEQ = 256
DEPTH = 2
DEC_BATCH = 2
DEC_SEQ = 2048
PAST_LEN = 256

GRID_W = 64
D_MIX = 1024
ATTN_HEADS = 8
ATTN_KV_HEADS = 2
ATTN_HEAD_DIM = 64
ATTN_WINDOW = 128
ATTN_BLOCK = 128
ROPE_THETA = 10000.0
MLSTM_HEADS = 4
MLSTM_HEAD_DIM = 64
MLSTM_CHUNK = 64
FORGET_BIAS = 3.0
POOL_GROUPS = 4
POOL_GROUP_DIM = 64
POOL_WINDOWS = (2, 4, 8, 16)
PEER_HEADS = 8
PEER_N_KEYS = 128
PEER_N_EXPERTS = PEER_N_KEYS * PEER_N_KEYS
PEER_QUERY_DIM = 256
PEER_HALF = PEER_QUERY_DIM // 2
PEER_TOPK = 16
PEER_TOKEN_BLOCK = 128
NORM_EPS = 1e-6

ATTN_Q = ATTN_HEADS * ATTN_HEAD_DIM
ATTN_KV = ATTN_KV_HEADS * ATTN_HEAD_DIM
MLSTM_W = MLSTM_HEADS * MLSTM_HEAD_DIM
POOL_W = POOL_GROUPS * POOL_GROUP_DIM
N_GATES = 4 * MLSTM_HEADS
D_IN = ATTN_Q + 2 * ATTN_KV + 4 * MLSTM_W + N_GATES + POOL_W

kernel_name = 'hybrid_dit_mlstm_pool_swa_peer_step'

F32 = jnp.float32


def rmsnorm(x, g):
    xf = x.astype(F32)
    y = xf * lax.rsqrt(jnp.mean(xf * xf, axis=-1, keepdims=True) + NORM_EPS)
    return (y * g.astype(F32)).astype(x.dtype)


def modulation(cond, w_mod, b_mod):
    m = jax.nn.silu(cond) @ w_mod + b_mod
    return tuple(part[..., None, :] for part in jnp.split(m, 6, axis=-1))


def axial_rope(x, row, col):
    half = ATTN_HEAD_DIM // 2
    inv = ROPE_THETA ** (-jnp.arange(0, half, 2, dtype=F32) / half)

    def rot(xa, pos):
        ang = pos.astype(F32)[:, None] * inv[None, :]
        cos = jnp.cos(ang)[None, :, None, :]
        sin = jnp.sin(ang)[None, :, None, :]
        x1, x2 = jnp.split(xa.astype(F32), 2, axis=-1)
        return jnp.concatenate([x1 * cos - x2 * sin, x2 * cos + x1 * sin], axis=-1)

    xr, xc = jnp.split(x, 2, axis=-1)
    return jnp.concatenate([rot(xr, row), rot(xc, col)], axis=-1).astype(x.dtype)


def softmax_with_sink(logits, sink):
    sink_col = jnp.broadcast_to(sink, logits.shape[:-1] + (1,))
    p = jax.nn.softmax(jnp.concatenate([logits, sink_col], axis=-1), axis=-1)
    return p[..., :-1]


def context_attention(q, k, v, sink):
    B, S = q.shape[:2]
    G = ATTN_HEADS // ATTN_KV_HEADS
    nb = S // ATTN_BLOCK
    scale = ATTN_HEAD_DIM ** -0.5
    qb = q.reshape(B, nb, ATTN_BLOCK, ATTN_KV_HEADS, G, ATTN_HEAD_DIM).swapaxes(0, 1)
    sink_l = sink.astype(F32).reshape(ATTN_KV_HEADS, G, 1, 1)

    def block(qi):
        s = jnp.einsum('bqkgd,bskd->bkgqs', qi, k).astype(F32) * scale
        p = softmax_with_sink(s, sink_l)
        return jnp.einsum('bkgqs,bskd->bqkgd', p.astype(v.dtype), v)

    o = lax.map(block, qb)
    return o.swapaxes(0, 1).reshape(B, S, ATTN_Q)


def latent_attention(q, k, v, kc, vc, sink):
    B, L = q.shape[:2]
    G = ATTN_HEADS // ATTN_KV_HEADS
    nb = L // ATTN_BLOCK
    nband = 3 * ATTN_BLOCK
    scale = ATTN_HEAD_DIM ** -0.5
    qb = q.reshape(B, nb, ATTN_BLOCK, ATTN_KV_HEADS, G, ATTN_HEAD_DIM)

    def band(t):
        tp = jnp.pad(t, ((0, 0), (ATTN_BLOCK, ATTN_BLOCK), (0, 0), (0, 0)))
        tp = tp.reshape(B, nb + 2, ATTN_BLOCK, ATTN_KV_HEADS, ATTN_HEAD_DIM)
        return jnp.concatenate([tp[:, :-2], tp[:, 1:-1], tp[:, 2:]], axis=2)

    kb, vb = band(k), band(v)
    blk = jnp.arange(nb)[:, None] * ATTN_BLOCK
    qpos = (blk + jnp.arange(ATTN_BLOCK)[None, :])[:, :, None]
    kpos = (blk - ATTN_BLOCK + jnp.arange(nband)[None, :])[:, None, :]
    valid = (jnp.abs(qpos - kpos) <= ATTN_WINDOW) & (kpos >= 0) & (kpos < L)
    s_band = jnp.einsum('bnqkgd,bnskd->bnkgqs', qb, kb).astype(F32) * scale
    s_band = jnp.where(valid[None, :, None, None], s_band, -jnp.inf)
    s_ctx = jnp.einsum('bnqkgd,bpkd->bnkgqp', qb, kc).astype(F32) * scale
    sink_l = sink.astype(F32).reshape(ATTN_KV_HEADS, G, 1, 1)
    p = softmax_with_sink(jnp.concatenate([s_band, s_ctx], axis=-1), sink_l)
    o = (jnp.einsum('bnkgqs,bnskd->bnqkgd', p[..., :nband].astype(vb.dtype), vb)
         + jnp.einsum('bnkgqp,bpkd->bnqkgd', p[..., nband:].astype(vc.dtype), vc))
    return o.reshape(B, L, ATTN_Q)


def mlstm_chunkwise(q, k, v, ig, lf, init):
    B, H, L, _ = q.shape
    nc = L // MLSTM_CHUNK

    def chunks(t):
        return jnp.moveaxis(t.reshape((B, H, nc, MLSTM_CHUNK) + t.shape[3:]), 2, 0)

    causal = jnp.tril(jnp.ones((MLSTM_CHUNK, MLSTM_CHUNK), bool))

    def step(carry, xs):
        C, n, m = carry
        qc, kc, vc, igc, lfc = xs
        b = jnp.cumsum(lfc, axis=-1)
        dlog = jnp.where(causal, b[..., :, None] - b[..., None, :] + igc[..., None, :], -jnp.inf)
        state_log = b + m[..., None]
        m_t = jnp.maximum(state_log, jnp.max(dlog, axis=-1))
        w = jnp.exp(dlog - m_t[..., None]) * jnp.einsum('bhtd,bhsd->bhts', qc, kc)
        sc = jnp.exp(state_log - m_t)
        num = sc[..., None] * jnp.einsum('bhtd,bhde->bhte', qc, C) + jnp.einsum('bhts,bhse->bhte', w, vc)
        nq = sc * jnp.einsum('bhtd,bhd->bht', qc, n) + jnp.sum(w, axis=-1)
        h = num / jnp.maximum(jnp.abs(nq), jnp.exp(-m_t))[..., None]
        b_last = b[..., -1]
        wlog = b_last[..., None] - b + igc
        m_new = jnp.maximum(b_last + m, jnp.max(wlog, axis=-1))
        decay = jnp.exp(b_last + m - m_new)
        wk = jnp.exp(wlog - m_new[..., None])[..., None] * kc
        C_new = decay[..., None, None] * C + jnp.einsum('bhsd,bhse->bhde', wk, vc)
        n_new = decay[..., None] * n + jnp.sum(wk, axis=2)
        return (C_new, n_new, m_new), h

    final, h = lax.scan(step, init, (chunks(q), chunks(k), chunks(v), chunks(ig), chunks(lf)))
    h = jnp.moveaxis(h, 0, 2).reshape(B, H, L, v.shape[-1])
    return h, final


def mlstm_bidirectional(q, k, v, gates, init_f, init_b):
    ig_f, fg_f, ig_b, fg_b = [jnp.moveaxis(g, -1, 1) for g in jnp.split(gates, 4, axis=-1)]
    h_f, fin_f = mlstm_chunkwise(q, k, v, ig_f, jax.nn.log_sigmoid(fg_f), init_f)

    def flip(t):
        return jnp.flip(t, axis=2)

    h_b, fin_b = mlstm_chunkwise(flip(q), flip(k), flip(v), flip(ig_b), flip(jax.nn.log_sigmoid(fg_b)), init_b)
    return h_f + flip(h_b), fin_f, fin_b


def multiscale_pool(x, pool_w, pool_scale):
    B, L, _ = x.shape
    xf = x.astype(F32)
    cs = jnp.concatenate([jnp.zeros((B, 1, POOL_W), F32), jnp.cumsum(xf, axis=1)], axis=1)
    t = jnp.arange(L)
    outs = []
    for g, w in enumerate(POOL_WINDOWS):
        lo = jnp.clip(t - w // 2, 0, L)
        hi = jnp.clip(t + w // 2, 0, L)
        sl = slice(g * POOL_GROUP_DIM, (g + 1) * POOL_GROUP_DIM)
        mean = (cs[:, hi, sl] - cs[:, lo, sl]) / (hi - lo).astype(F32)[None, :, None]
        outs.append((mean - xf[..., sl]).astype(x.dtype) @ pool_w[g])
    return jnp.concatenate(outs, axis=-1) * pool_scale


def peer_ffn(x, w_q, sub_keys, u_table, v_table):
    B, L, D = x.shape
    T = B * L
    xt = x.reshape(T, D)
    q = (xt @ w_q).reshape(T, PEER_HEADS, 2, PEER_HALF)
    s = jnp.einsum('thcd,hckd->thck', q, sub_keys).astype(F32)
    top_s, top_i = lax.top_k(s, PEER_TOPK)
    cand = (top_s[:, :, 0, :, None] + top_s[:, :, 1, None, :]).reshape(T, PEER_HEADS, PEER_TOPK * PEER_TOPK)
    best_s, best_p = lax.top_k(cand, PEER_TOPK)
    i1 = jnp.take_along_axis(top_i[:, :, 0, :], best_p // PEER_TOPK, axis=-1)
    i2 = jnp.take_along_axis(top_i[:, :, 1, :], best_p % PEER_TOPK, axis=-1)
    expert = i1 * PEER_N_KEYS + i2
    gate = jax.nn.softmax(best_s, axis=-1).astype(x.dtype)
    nblk = T // PEER_TOKEN_BLOCK

    def block(args):
        xb, eb, gb = args
        a = jax.nn.gelu(jnp.einsum('thkd,td->thk', u_table[eb], xb), approximate=False)
        return jnp.einsum('thk,thkd->td', gb * a, v_table[eb])

    out = lax.map(block, (xt.reshape(nblk, PEER_TOKEN_BLOCK, D),
                          expert.reshape(nblk, PEER_TOKEN_BLOCK, PEER_HEADS, PEER_TOPK),
                          gate.reshape(nblk, PEER_TOKEN_BLOCK, PEER_HEADS, PEER_TOPK)))
    return out.reshape(B, L, D)


def mixing_sublayer(h, p, ctx_kv, mlstm_init, rope_pos):
    B, L, _ = h.shape
    sizes = (ATTN_Q, ATTN_KV, ATTN_KV, MLSTM_W, MLSTM_W, MLSTM_W, MLSTM_W, N_GATES, POOL_W)
    offs = []
    acc = 0
    for sz in sizes[:-1]:
        acc += sz
        offs.append(acc)
    q_a, k_a, v_a, q_m, k_m, v_m, o_m, g_m, x_p = jnp.split(h @ p['w_in'], offs, axis=-1)

    qa = q_a.reshape(B, L, ATTN_HEADS, ATTN_HEAD_DIM)
    ka = k_a.reshape(B, L, ATTN_KV_HEADS, ATTN_HEAD_DIM)
    va = v_a.reshape(B, L, ATTN_KV_HEADS, ATTN_HEAD_DIM)
    if ctx_kv is None:
        attn = context_attention(qa, ka, va, p['attn_sink'])
        kv_out = (ka, va)
    else:
        row, col = rope_pos
        attn = latent_attention(axial_rope(qa, row, col), axial_rope(ka, row, col), va,
                                ctx_kv[0], ctx_kv[1], p['attn_sink'])
        kv_out = None

    def heads(t):
        return t.reshape(B, L, MLSTM_HEADS, MLSTM_HEAD_DIM).transpose(0, 2, 1, 3).astype(F32)

    gates = g_m.astype(F32) + p['gate_b'].astype(F32)
    h_m, fin_f, fin_b = mlstm_bidirectional(heads(q_m), heads(k_m) * MLSTM_HEAD_DIM ** -0.5, heads(v_m),
                                            gates, mlstm_init[0], mlstm_init[1])
    h_m = h_m * lax.rsqrt(jnp.mean(h_m * h_m, axis=-1, keepdims=True) + NORM_EPS)
    h_m = h_m.transpose(0, 2, 1, 3).reshape(B, L, MLSTM_W) * p['mlstm_norm_g'].astype(F32)
    mlstm_out = (jax.nn.sigmoid(o_m.astype(F32)) * h_m).astype(h.dtype)

    pool_out = multiscale_pool(x_p, p['pool_w'], p['pool_scale'])

    out = jnp.concatenate([attn, mlstm_out, pool_out], axis=-1) @ p['w_out']
    return out, kv_out, (fin_f, fin_b)


def trunk_layer(x, mods, p, ctx_kv, mlstm_init, rope_pos):
    sh1, sc1, g1, sh2, sc2, g2 = mods
    h = rmsnorm(x, p['norm1_g']) * (1 + sc1) + sh1
    y, kv, fins = mixing_sublayer(h, p, ctx_kv, mlstm_init, rope_pos)
    x = x + g1 * y
    h = rmsnorm(x, p['norm2_g']) * (1 + sc2) + sh2
    x = x + g2 * peer_ffn(h, p['peer_wq'], p['peer_keys'], p['peer_u'], p['peer_v'])
    return x, kv, fins


def setup_inputs(seed: int = 0) -> dict:
    key = jax.random.key(seed)
    ks = jax.random.split(key, 26)

    def nrm(k, shape, s=1.0):
        return s * jax.random.normal(k, shape, F32)

    gate_offset = jnp.repeat(jnp.array([0.0, FORGET_BIAS, 0.0, FORGET_BIAS], F32), MLSTM_HEADS)
    return {
        'x_prompt': nrm(ks[0], (BATCH, SEQ, D_MODEL)),
        'x_sample': nrm(ks[1], (DEC_BATCH, DEC_SEQ, D_MODEL)),
        'cache_k': nrm(ks[2], (DEC_BATCH, DEPTH, PAST_LEN, ATTN_KV_HEADS, ATTN_HEAD_DIM)),
        'cache_v': nrm(ks[3], (DEC_BATCH, DEPTH, PAST_LEN, ATTN_KV_HEADS, ATTN_HEAD_DIM)),
        'state_C': nrm(ks[4], (DEC_BATCH, DEPTH, 2, MLSTM_HEADS, MLSTM_HEAD_DIM, MLSTM_HEAD_DIM), 0.5),
        'state_n': nrm(ks[5], (DEC_BATCH, DEPTH, 2, MLSTM_HEADS, MLSTM_HEAD_DIM), 0.5),
        'state_m': nrm(ks[6], (DEC_BATCH, DEPTH, 2, MLSTM_HEADS)),
        'c': nrm(ks[7], (DEC_BATCH, D_MODEL)),
        'c_ctx': nrm(ks[8], (D_MODEL,)),
        'w_mod': nrm(ks[9], (DEPTH, D_MODEL, 6 * D_MODEL), 0.5 * D_MODEL ** -0.5),
        'b_mod': nrm(ks[10], (DEPTH, 6 * D_MODEL), 0.02),
        'norm1_g': 1.0 + nrm(ks[11], (DEPTH, D_MODEL), 0.05),
        'norm2_g': 1.0 + nrm(ks[12], (DEPTH, D_MODEL), 0.05),
        'w_in': nrm(ks[13], (DEPTH, D_MODEL, D_IN), D_MODEL ** -0.5),
        'gate_b': gate_offset + nrm(ks[14], (DEPTH, N_GATES), 0.1),
        'attn_sink': nrm(ks[15], (DEPTH, ATTN_HEADS)),
        'mlstm_norm_g': 1.0 + nrm(ks[16], (DEPTH, MLSTM_W), 0.05),
        'pool_w': nrm(ks[17], (DEPTH, POOL_GROUPS, POOL_GROUP_DIM, POOL_GROUP_DIM), POOL_GROUP_DIM ** -0.5),
        'pool_scale': 1.0 + nrm(ks[18], (DEPTH, POOL_W), 0.05),
        'w_out': nrm(ks[19], (DEPTH, D_MIX, D_MODEL), D_MIX ** -0.5),
        'peer_wq': nrm(ks[20], (DEPTH, D_MODEL, PEER_HEADS * PEER_QUERY_DIM), D_MODEL ** -0.5),
        'peer_keys': nrm(ks[21], (DEPTH, PEER_HEADS, 2, PEER_N_KEYS, PEER_HALF), PEER_HALF ** -0.5),
        'peer_u': nrm(ks[22], (DEPTH, PEER_N_EXPERTS, D_MODEL), D_MODEL ** -0.5),
        'peer_v': nrm(ks[23], (DEPTH, PEER_N_EXPERTS, D_MODEL), (PEER_HEADS * PEER_TOPK) ** -0.5),
        'final_norm_g': 1.0 + nrm(ks[24], (D_MODEL,), 0.05),
    }


def reference(x_prompt, x_sample, cache_k, cache_v, state_C, state_n, state_m, c, c_ctx,
              w_mod, b_mod, norm1_g, norm2_g, w_in, gate_b, attn_sink, mlstm_norm_g, pool_w, pool_scale,
              w_out, peer_wq, peer_keys, peer_u, peer_v, final_norm_g):
    rows = x_sample.shape[1] // GRID_W
    row = jnp.repeat(jnp.arange(rows), GRID_W)
    col = jnp.tile(jnp.arange(GRID_W), rows)
    B = x_prompt.shape[0]
    zero_state = (jnp.zeros((B, MLSTM_HEADS, MLSTM_HEAD_DIM, MLSTM_HEAD_DIM), F32),
                  jnp.zeros((B, MLSTM_HEADS, MLSTM_HEAD_DIM), F32),
                  jnp.zeros((B, MLSTM_HEADS), F32))
    xp, xs = x_prompt, x_sample
    ks_, vs_, Cs_, ns_, ms_ = [], [], [], [], []
    for l in range(DEPTH):
        p = {'w_in': w_in[l], 'gate_b': gate_b[l], 'attn_sink': attn_sink[l], 'mlstm_norm_g': mlstm_norm_g[l],
             'pool_w': pool_w[l], 'pool_scale': pool_scale[l], 'w_out': w_out[l],
             'norm1_g': norm1_g[l], 'norm2_g': norm2_g[l], 'peer_wq': peer_wq[l], 'peer_keys': peer_keys[l],
             'peer_u': peer_u[l], 'peer_v': peer_v[l]}
        xp, (k_l, v_l), (fin_f, fin_b) = trunk_layer(xp, modulation(c_ctx, w_mod[l], b_mod[l]), p,
                                                      None, (zero_state, zero_state), None)
        ks_.append(k_l)
        vs_.append(v_l)
        Cs_.append(jnp.stack([fin_f[0], fin_b[0]], axis=1))
        ns_.append(jnp.stack([fin_f[1], fin_b[1]], axis=1))
        ms_.append(jnp.stack([fin_f[2], fin_b[2]], axis=1))
        init = tuple((state_C[:, l, d].astype(F32), state_n[:, l, d].astype(F32), state_m[:, l, d].astype(F32))
                     for d in range(2))
        xs, _, _ = trunk_layer(xs, modulation(c, w_mod[l], b_mod[l]), p,
                               (cache_k[:, l], cache_v[:, l]), init, (row, col))
    y_prompt = rmsnorm(xp, final_norm_g)
    y_sample = rmsnorm(xs, final_norm_g)
    return (y_prompt, y_sample, jnp.stack(ks_, axis=1), jnp.stack(vs_, axis=1),
            jnp.stack(Cs_, axis=1), jnp.stack(ns_, axis=1), jnp.stack(ms_, axis=1))
```

```python
import functools

import numpy as np
import jax
import jax.numpy as jnp
from jax import lax
from jax.experimental import pallas as pl
from jax.experimental.pallas import tpu as pltpu

F32 = jnp.float32
BF16 = jnp.bfloat16
I32 = jnp.int32

D_MODEL = 1024
DEPTH = 2
GRID_W = 64
HEAD = 64
HP = 128
ATTN_HEADS = 8
ATTN_KV_HEADS = 2
ATTN_GROUP = ATTN_HEADS // ATTN_KV_HEADS
ATTN_BLOCK = 128
ROPE_THETA = 10000.0
ML_HEADS = 4
ML_CHUNK = 128
POOL_GROUPS = 4
POOL_WINDOWS = (2, 4, 8, 16)
N_GATES = 4 * ML_HEADS
PEER_HEADS = 8
PEER_KEYS = 128
PEER_HALF = 128
PEER_TOPK = 16
PEER_PICKS = PEER_HEADS * PEER_TOPK
PEER_EXPERTS = PEER_KEYS * PEER_KEYS
NORM_EPS = 1e-6
INV_SQRT2 = 0.7071067811865476

N_QA = ATTN_HEADS * HP
N_KA = ATTN_KV_HEADS * HP
N_ML = ML_HEADS * HP
N_XP = POOL_GROUPS * HP
OFF_QA = 0
OFF_KA = OFF_QA + N_QA
OFF_VA = OFF_KA + N_KA
OFF_QAS = OFF_VA + N_KA
OFF_KAS = OFF_QAS + N_QA
OFF_QM = OFF_KAS + N_KA
OFF_KM = OFF_QM + N_ML
OFF_VM = OFF_KM + N_ML
OFF_OM = OFF_VM + N_ML
OFF_GM = OFF_OM + N_ML
OFF_XP = OFF_GM + HP
N_PROJ = OFF_XP + N_XP
N_MIX = N_QA + N_ML + N_XP

TM = 256
PEER_TB = 512
PEER_EB = 256
VMEM_LIMIT = 56 * 1024 * 1024


def _cparams(sem):
    return pltpu.CompilerParams(dimension_semantics=sem, vmem_limit_bytes=VMEM_LIMIT)


def _mod_kernel(c_ref, w_ref, b_ref, o_ref):
    c = c_ref[...]
    s = c * jax.nn.sigmoid(c)
    o_ref[0] = jnp.dot(s, w_ref[0], precision=lax.Precision.HIGHEST,
                       preferred_element_type=F32) + b_ref[0]


def _modulation(cond, w_mod, b_mod):
    nb = 6
    return pl.pallas_call(
        _mod_kernel,
        grid=(DEPTH, nb),
        in_specs=[pl.BlockSpec((8, D_MODEL), lambda l, j: (0, 0)),
                  pl.BlockSpec((1, D_MODEL, D_MODEL), lambda l, j: (l, 0, j)),
                  pl.BlockSpec((1, 1, D_MODEL), lambda l, j: (l, 0, j))],
        out_specs=pl.BlockSpec((1, 8, D_MODEL), lambda l, j: (l, 0, j)),
        out_shape=jax.ShapeDtypeStruct((DEPTH, 8, 6 * D_MODEL), F32),
        compiler_params=_cparams(("arbitrary", "arbitrary")),
        name="modulation",
    )(cond, w_mod, b_mod.reshape(DEPTH, 1, 6 * D_MODEL))


def _pre_kernel(x_ref, mod_ref, g_ref, w_ref, gb_ref, cos_ref, sin_ref,
                qa_ref, ka_ref, va_ref, qm_ref, km_ref, vm_ref, om_ref, gm_ref, xp_ref):
    x = x_ref[...]
    y = x * lax.rsqrt(jnp.mean(x * x, axis=-1, keepdims=True) + NORM_EPS) * g_ref[...]
    mod = mod_ref[0]
    hb = (y * (1.0 + mod[1:2]) + mod[0:1]).astype(BF16)

    def proj(off, n):
        return jnp.dot(hb, w_ref[:, off:off + n], preferred_element_type=F32)

    cos = cos_ref[...]
    sin = sin_ref[...]
    for h in range(ATTN_HEADS):
        qa_ref[:, h * HP:(h + 1) * HP] = (proj(OFF_QA + h * HP, HP) * cos
                                          + proj(OFF_QAS + h * HP, HP) * sin)
    for h in range(ATTN_KV_HEADS):
        ka_ref[:, h * HP:(h + 1) * HP] = (proj(OFF_KA + h * HP, HP) * cos
                                          + proj(OFF_KAS + h * HP, HP) * sin)
    va_ref[...] = proj(OFF_VA, N_KA)
    qm_ref[...] = proj(OFF_QM, N_ML)
    km_ref[...] = proj(OFF_KM, N_ML)
    vm_ref[...] = proj(OFF_VM, N_ML)
    om_ref[...] = proj(OFF_OM, N_ML)
    gm_ref[...] = proj(OFF_GM, HP) + gb_ref[...]
    xp_ref[...] = proj(OFF_XP, N_XP)


def _pre(x, mod, mod_row, norm_g, w_proj, gate_b, cos, sin):
    T = x.shape[0]
    tok = lambda n: pl.BlockSpec((TM, n), lambda i: (i, 0))
    full = lambda a: pl.BlockSpec(a.shape, lambda i: (0,) * a.ndim)
    widths = (N_QA, N_KA, N_KA, N_ML, N_ML, N_ML, N_ML, HP, N_XP)
    return pl.pallas_call(
        _pre_kernel,
        grid=(T // TM,),
        in_specs=[tok(D_MODEL),
                  pl.BlockSpec((1, 8, D_MODEL), lambda i: (mod_row(i), 0, 0)),
                  full(norm_g), full(w_proj), full(gate_b), tok(HP), tok(HP)],
        out_specs=[tok(n) for n in widths],
        out_shape=[jax.ShapeDtypeStruct((T, n), F32) for n in widths],
        compiler_params=_cparams(("arbitrary",)),
        name="pre_mix",
    )(x, mod, norm_g, w_proj, gate_b, cos, sin)


def _softmax_sink_pv(scores, values, sink):
    m = sink
    for s in scores:
        m = jnp.maximum(m, jnp.max(s, axis=-1, keepdims=True))
    den = jnp.exp(sink - m)
    ps = []
    for s in scores:
        p = jnp.exp(s - m)
        den = den + jnp.sum(p, axis=-1, keepdims=True)
        ps.append(p)
    inv = 1.0 / den
    out = None
    for p, v in zip(ps, values):
        o = jnp.dot((p * inv).astype(BF16), v, preferred_element_type=F32)
        out = o if out is None else out + o
    return out


def _qk(q, k):
    return lax.dot_general(q, k, (((1,), (1,)), ((), ())), preferred_element_type=F32)


def _ctx_attn_kernel(q_ref, k_ref, v_ref, sink_ref, o_ref):
    for h in range(ATTN_HEADS):
        g = h // ATTN_GROUP
        q = q_ref[:, h * HP:(h + 1) * HP].astype(BF16)
        k = k_ref[:, g * HP:(g + 1) * HP].astype(BF16)
        v = v_ref[:, g * HP:(g + 1) * HP].astype(BF16)
        sink = jnp.max(sink_ref[h:h + 1, :], axis=1, keepdims=True)
        o_ref[:, h * HP:(h + 1) * HP] = _softmax_sink_pv([_qk(q, k)], [v], sink)


def _lat_attn_kernel(q_ref, k_ref, v_ref, kc_ref, vc_ref, sink_ref, o_ref, *, nb):
    n = pl.program_id(1)
    ri = lax.broadcasted_iota(I32, (ATTN_BLOCK, ATTN_BLOCK), 0)
    ci = lax.broadcasted_iota(I32, (ATTN_BLOCK, ATTN_BLOCK), 1)
    nv = jnp.full((ATTN_BLOCK, ATTN_BLOCK), n, I32)
    ok_prev = jnp.logical_and(ri <= ci, nv >= 1)
    ok_next = jnp.logical_and(ci <= ri, nv + 1 < nb)
    r_prev = pl.multiple_of(jnp.maximum(n - 1, 0) * ATTN_BLOCK, ATTN_BLOCK)
    r_cur = pl.multiple_of(n * ATTN_BLOCK, ATTN_BLOCK)
    r_next = pl.multiple_of(jnp.minimum(n + 1, nb - 1) * ATTN_BLOCK, ATTN_BLOCK)
    for h in range(ATTN_HEADS):
        g = h // ATTN_GROUP
        gs = slice(g * HP, (g + 1) * HP)
        q = q_ref[:, h * HP:(h + 1) * HP].astype(BF16)
        ks = [k_ref[pl.ds(r, ATTN_BLOCK), gs].astype(BF16) for r in (r_prev, r_cur, r_next)]
        vs = [v_ref[pl.ds(r, ATTN_BLOCK), gs].astype(BF16) for r in (r_prev, r_cur, r_next)]
        s_prev = jnp.where(ok_prev, _qk(q, ks[0]), -jnp.inf)
        s_cur = _qk(q, ks[1])
        s_next = jnp.where(ok_next, _qk(q, ks[2]), -jnp.inf)
        s_ctx = _qk(q, kc_ref[0, :, gs])
        sink = jnp.max(sink_ref[h:h + 1, :], axis=1, keepdims=True)
        o_ref[:, h * HP:(h + 1) * HP] = _softmax_sink_pv(
            [s_prev, s_cur, s_next, s_ctx], vs + [vc_ref[0, :, gs]], sink)


def _ctx_attention(qa, ka, va, sink, batch, seq):
    blk = lambda n: pl.BlockSpec((seq, n), lambda b: (b, 0))
    return pl.pallas_call(
        _ctx_attn_kernel,
        grid=(batch,),
        in_specs=[blk(N_QA), blk(N_KA), blk(N_KA), pl.BlockSpec(sink.shape, lambda b: (0, 0))],
        out_specs=blk(N_QA),
        out_shape=jax.ShapeDtypeStruct((batch * seq, N_QA), F32),
        compiler_params=_cparams(("arbitrary",)),
        name="ctx_attention",
    )(qa, ka, va, sink)


def _lat_attention(qa, ka, va, kc, vc, sink, batch, seq, row0):
    nb = seq // ATTN_BLOCK
    b0 = row0 // seq
    q0 = row0 // ATTN_BLOCK
    return pl.pallas_call(
        functools.partial(_lat_attn_kernel, nb=nb),
        grid=(batch, nb),
        in_specs=[pl.BlockSpec((ATTN_BLOCK, N_QA), lambda b, n: (q0 + b * nb + n, 0)),
                  pl.BlockSpec((seq, N_KA), lambda b, n: (b0 + b, 0)),
                  pl.BlockSpec((seq, N_KA), lambda b, n: (b0 + b, 0)),
                  pl.BlockSpec((1,) + kc.shape[1:], lambda b, n: (b, 0, 0)),
                  pl.BlockSpec((1,) + vc.shape[1:], lambda b, n: (b, 0, 0)),
                  pl.BlockSpec(sink.shape, lambda b, n: (0, 0))],
        out_specs=pl.BlockSpec((ATTN_BLOCK, N_QA), lambda b, n: (b * nb + n, 0)),
        out_shape=jax.ShapeDtypeStruct((batch * seq, N_QA), F32),
        compiler_params=_cparams(("arbitrary", "arbitrary")),
        name="lat_attention",
    )(qa, ka, va, kc, vc, sink)


def _mlstm_kernel(q_ref, k_ref, v_ref, gm_ref, gmt_ref, c0_ref, n0_ref, m0_ref,
                  h_ref, cf_ref, nf_ref, mf_ref, c_s, n_s, m_s, *, rev, nc):
    C = ML_CHUNK
    ri = lax.broadcasted_iota(I32, (C, C), 0)
    ci = lax.broadcasted_iota(I32, (C, C), 1)
    incl = (ci >= ri) if rev else (ci <= ri)
    incl_t = (ri >= ci) if rev else (ri <= ci)
    gbase = 8 if rev else 0
    c_s[...] = c0_ref[0]
    n_s[...] = n0_ref[0]
    m_s[...] = m0_ref[0]

    def step(i, carry):
        c = (nc - 1 - i) if rev else i
        r0 = pl.multiple_of(c * C, C)
        gm = gm_ref[pl.ds(r0, C), :]
        gmt = gmt_ref[:, pl.ds(r0, C)]
        for hd in range(ML_HEADS):
            hs = slice(hd * HP, (hd + 1) * HP)
            gi, gf = gbase + hd, gbase + ML_HEADS + hd
            ig_col = jnp.sum(jnp.where(ci == gi, gm, 0.0), axis=1, keepdims=True)
            fg_col = jnp.sum(jnp.where(ci == gf, gm, 0.0), axis=1, keepdims=True)
            ig_row = gmt[gi:gi + 1, :]
            lf_col = jax.nn.log_sigmoid(fg_col)
            lf_row = jax.nn.log_sigmoid(gmt[gf:gf + 1, :])
            b_col = jnp.sum(jnp.where(incl, lf_row, 0.0), axis=1, keepdims=True)
            b_row = jnp.sum(jnp.where(incl_t, lf_col, 0.0), axis=0, keepdims=True)
            b_last = jnp.sum(lf_row, axis=1, keepdims=True)
            m_prev = m_s[hd][0:1, 0:1]
            n_prev = n_s[hd][0:1, :]
            c_prev = c_s[hd]
            q = q_ref[pl.ds(r0, C), hs]
            k = k_ref[pl.ds(r0, C), hs]
            v = v_ref[pl.ds(r0, C), hs].astype(BF16)
            dlog = jnp.where(incl, b_col - b_row + ig_row, -jnp.inf)
            state_log = b_col + m_prev
            m_t = jnp.maximum(state_log, jnp.max(dlog, axis=1, keepdims=True))
            w = jnp.exp(dlog - m_t) * _qk(q.astype(BF16), k.astype(BF16))
            sc = jnp.exp(state_log - m_t)
            num = (sc * jnp.dot(q.astype(BF16), c_prev.astype(BF16), preferred_element_type=F32)
                   + jnp.dot(w.astype(BF16), v, preferred_element_type=F32))
            nq = sc * jnp.sum(q * n_prev, axis=1, keepdims=True) + jnp.sum(w, axis=1, keepdims=True)
            h_ref[pl.ds(r0, C), hs] = num / jnp.maximum(jnp.abs(nq), jnp.exp(-m_t))
            wlog = b_last - b_col + ig_col
            m_new = jnp.maximum(b_last + m_prev, jnp.max(wlog, axis=0, keepdims=True))
            decay = jnp.exp(b_last + m_prev - m_new)
            wk = jnp.exp(wlog - m_new) * k
            c_s[hd] = decay * c_prev + lax.dot_general(
                wk.astype(BF16), v, (((0,), (0,)), ((), ())), preferred_element_type=F32)
            n_s[hd] = jnp.broadcast_to(decay * n_prev + jnp.sum(wk, axis=0, keepdims=True), (8, HP))
            m_s[hd] = jnp.broadcast_to(m_new, (8, HP))
        return carry

    lax.fori_loop(0, nc, step, 0)
    cf_ref[0] = c_s[...]
    nf_ref[0] = n_s[...]
    mf_ref[0] = m_s[...]


def _mlstm(qm, km, vm, gm, gmt, c0, n0, m0, batch, seq, row0, rev):
    nc = seq // ML_CHUNK
    b0 = row0 // seq
    blk = lambda n: pl.BlockSpec((seq, n), lambda b: (b0 + b, 0))
    st_c = pl.BlockSpec((1, ML_HEADS, HP, HP), lambda b: (b, 0, 0, 0))
    st_v = pl.BlockSpec((1, ML_HEADS, 8, HP), lambda b: (b, 0, 0, 0))
    return pl.pallas_call(
        functools.partial(_mlstm_kernel, rev=rev, nc=nc),
        grid=(batch,),
        in_specs=[blk(N_ML), blk(N_ML), blk(N_ML), blk(HP),
                  pl.BlockSpec((N_GATES, seq), lambda b: (0, b0 + b)), st_c, st_v, st_v],
        out_specs=[pl.BlockSpec((seq, N_ML), lambda b: (b, 0)), st_c, st_v, st_v],
        out_shape=[jax.ShapeDtypeStruct((batch * seq, N_ML), F32),
                   jax.ShapeDtypeStruct((batch, ML_HEADS, HP, HP), F32),
                   jax.ShapeDtypeStruct((batch, ML_HEADS, 8, HP), F32),
                   jax.ShapeDtypeStruct((batch, ML_HEADS, 8, HP), F32)],
        scratch_shapes=[pltpu.VMEM((ML_HEADS, HP, HP), F32),
                        pltpu.VMEM((ML_HEADS, 8, HP), F32),
                        pltpu.VMEM((ML_HEADS, 8, HP), F32)],
        compiler_params=_cparams(("arbitrary",)),
        name="mlstm_bwd" if rev else "mlstm_fwd",
    )(qm, km, vm, gm, gmt, c0, n0, m0)


def _pool_kernel(x_ref, w_ref, s_ref, o_ref, *, seq):
    t = lax.broadcasted_iota(I32, (seq, HP), 0)
    for g, win in enumerate(POOL_WINDOWS):
        gs = slice(g * HP, (g + 1) * HP)
        x = x_ref[:, gs]
        half = win // 2
        tot = jnp.zeros_like(x)
        for d in range(-half, half):
            if d == 0:
                tot = tot + x
            else:
                shifted = pltpu.roll(x, (-d) % seq, 0)
                ok = jnp.logical_and(t + d >= 0, t + d < seq)
                tot = tot + jnp.where(ok, shifted, 0.0)
        cnt = (jnp.minimum(t + half, seq) - jnp.maximum(t - half, 0)).astype(F32)
        diff = tot / cnt - x
        o_ref[:, gs] = jnp.dot(diff.astype(BF16), w_ref[g], preferred_element_type=F32) * s_ref[:, gs]


def _pool(xp, pool_w, pool_scale, batch, seq, row0):
    b0 = row0 // seq
    return pl.pallas_call(
        functools.partial(_pool_kernel, seq=seq),
        grid=(batch,),
        in_specs=[pl.BlockSpec((seq, N_XP), lambda b: (b0 + b, 0)),
                  pl.BlockSpec(pool_w.shape, lambda b: (0, 0, 0)),
                  pl.BlockSpec(pool_scale.shape, lambda b: (0, 0))],
        out_specs=pl.BlockSpec((seq, N_XP), lambda b: (b, 0)),
        out_shape=jax.ShapeDtypeStruct((batch * seq, N_XP), F32),
        compiler_params=_cparams(("arbitrary",)),
        name="pool",
    )(xp, pool_w, pool_scale)


def _post_kernel(x_ref, attn_ref, hf_ref, hb_ref, om_ref, pool_ref, mod_ref, mg_ref, n2_ref,
                 wo_ref, wq_ref, keys_ref, x1_ref, h2_ref, st_ref):
    mod = mod_ref[0]
    hm = hf_ref[...] + hb_ref[...]
    parts = []
    for hd in range(ML_HEADS):
        hh = hm[:, hd * HP:(hd + 1) * HP]
        ms = jnp.sum(hh * hh, axis=-1, keepdims=True) * (1.0 / HEAD)
        parts.append(hh * lax.rsqrt(ms + NORM_EPS))
    ml = jnp.concatenate(parts, axis=1) * mg_ref[...] * jax.nn.sigmoid(om_ref[...])
    y = (jnp.dot(attn_ref[...].astype(BF16), wo_ref[0:N_QA, :], preferred_element_type=F32)
         + jnp.dot(ml.astype(BF16), wo_ref[N_QA:N_QA + N_ML, :], preferred_element_type=F32)
         + jnp.dot(pool_ref[...].astype(BF16), wo_ref[N_QA + N_ML:N_MIX, :], preferred_element_type=F32))
    x1 = x_ref[...] + mod[2:3] * y
    x1_ref[...] = x1
    hn = x1 * lax.rsqrt(jnp.mean(x1 * x1, axis=-1, keepdims=True) + NORM_EPS) * n2_ref[...]
    h2 = (hn * (1.0 + mod[4:5]) + mod[3:4]).astype(BF16)
    h2_ref[...] = h2
    qt = lax.dot_general(wq_ref[...], h2, (((1,), (1,)), ((), ())), preferred_element_type=F32)
    for hc in range(2 * PEER_HEADS):
        sl = slice(hc * PEER_HALF, (hc + 1) * PEER_HALF)
        st_ref[sl, :] = jnp.dot(keys_ref[hc], qt[sl, :].astype(BF16), preferred_element_type=F32)


def _post(x, attn, hf, hb, om, pool, mod, mod_row, mnorm_g, norm2_g, w_out, wq_t, keys):
    T = x.shape[0]
    tok = lambda n: pl.BlockSpec((TM, n), lambda i: (i, 0))
    full = lambda a: pl.BlockSpec(a.shape, lambda i: (0,) * a.ndim)
    nq = wq_t.shape[0]
    return pl.pallas_call(
        _post_kernel,
        grid=(T // TM,),
        in_specs=[tok(D_MODEL), tok(N_QA), tok(N_ML), tok(N_ML), tok(N_ML), tok(N_XP),
                  pl.BlockSpec((1, 8, D_MODEL), lambda i: (mod_row(i), 0, 0)),
                  full(mnorm_g), full(norm2_g), full(w_out), full(wq_t), full(keys)],
        out_specs=[tok(D_MODEL), tok(D_MODEL), pl.BlockSpec((nq, TM), lambda i: (0, i))],
        out_shape=[jax.ShapeDtypeStruct((T, D_MODEL), F32),
                   jax.ShapeDtypeStruct((T, D_MODEL), BF16),
                   jax.ShapeDtypeStruct((nq, T), F32)],
        compiler_params=_cparams(("arbitrary",)),
        name="post_mix",
    )(x, attn, hf, hb, om, pool, mod, mnorm_g, norm2_g, w_out, wq_t, keys)


def _top16(s, idx, n):
    row = lax.broadcasted_iota(I32, (PEER_TOPK, 128), 0)

    def body(k, carry):
        s, ts, ti = carry
        m = jnp.max(s, axis=0, keepdims=True)
        am = jnp.min(jnp.where(s == m, idx, float(n)), axis=0, keepdims=True)
        ts = jnp.where(row == k, m, ts)
        ti = jnp.where(row == k, am, ti)
        return jnp.where(idx == am, -jnp.inf, s), ts, ti

    z = jnp.zeros((PEER_TOPK, 128), F32)
    _, ts, ti = lax.fori_loop(0, PEER_TOPK, body, (s, z, z))
    return ts, ti


def _pick_rows(sel, table):
    out = jnp.zeros_like(table)
    for a in range(PEER_TOPK):
        out = jnp.where(sel == a, table[a:a + 1, :], out)
    return out


def _topk_kernel(st_ref, i1_ref, i2_ref, g_ref):
    kidx = lax.broadcasted_iota(I32, (PEER_KEYS, 128), 0).astype(F32)
    cidx = lax.broadcasted_iota(I32, (PEER_TOPK * PEER_TOPK, 128), 0).astype(F32)

    def head(h, carry):
        r0 = pl.multiple_of(h * 2 * PEER_KEYS, 2 * PEER_KEYS)
        ts0, ti0 = _top16(st_ref[pl.ds(r0, PEER_KEYS), :], kidx, PEER_KEYS)
        ts1, ti1 = _top16(st_ref[pl.ds(r0 + PEER_KEYS, PEER_KEYS), :], kidx, PEER_KEYS)
        cand = jnp.concatenate([ts0[a:a + 1, :] + ts1 for a in range(PEER_TOPK)], axis=0)
        bs, bp = _top16(cand, cidx, PEER_TOPK * PEER_TOPK)
        bp = bp.astype(I32)
        i1 = _pick_rows(jnp.right_shift(bp, 4), ti0)
        i2 = _pick_rows(jnp.bitwise_and(bp, PEER_TOPK - 1), ti1)
        e = jnp.exp(bs - jnp.max(bs, axis=0, keepdims=True))
        o0 = pl.multiple_of(h * PEER_TOPK, PEER_TOPK)
        i1_ref[pl.ds(o0, PEER_TOPK), :] = i1.astype(I32)
        i2_ref[pl.ds(o0, PEER_TOPK), :] = i2.astype(I32)
        g_ref[pl.ds(o0, PEER_TOPK), :] = e / jnp.sum(e, axis=0, keepdims=True)
        return carry

    lax.fori_loop(0, PEER_HEADS, head, 0)


def _peer_topk(st):
    nq, T = st.shape
    blk = pl.BlockSpec((PEER_PICKS, 128), lambda i: (0, i))
    return pl.pallas_call(
        _topk_kernel,
        grid=(T // 128,),
        in_specs=[pl.BlockSpec((nq, 128), lambda i: (0, i))],
        out_specs=[blk, blk, blk],
        out_shape=[jax.ShapeDtypeStruct((PEER_PICKS, T), I32),
                   jax.ShapeDtypeStruct((PEER_PICKS, T), I32),
                   jax.ShapeDtypeStruct((PEER_PICKS, T), F32)],
        compiler_params=_cparams(("arbitrary",)),
        name="peer_topk",
    )(st)


def _peer_kernel(h2_ref, i1_ref, i2_ref, g_ref, u_ref, v_ref, x1_ref, mod_ref, o_ref, gs_ref, acc_ref):
    e = pl.program_id(1)
    TB = PEER_TB

    @pl.when(e == 0)
    def _build_gates():
        acc_ref[...] = jnp.zeros_like(acc_ref)
        sub = lax.broadcasted_iota(I32, (PEER_KEYS, PEER_PICKS), 0)

        def body(t, carry):
            i1 = i1_ref[pl.ds(t, 1), :]
            i2 = i2_ref[pl.ds(t, 1), :]
            g = g_ref[pl.ds(t, 1), :]
            r1 = jnp.where(sub == i1, g, 0.0).astype(BF16)
            r2 = jnp.where(sub == i2, 1.0, 0.0).astype(BF16)
            gt = lax.dot_general(r1, r2, (((1,), (1,)), ((), ())), preferred_element_type=F32)
            gs_ref[pl.ds(t, PEER_KEYS, stride=TB), :] = gt
            return carry

        lax.fori_loop(0, TB, body, 0)

    a = lax.dot_general(h2_ref[...], u_ref[...], (((1,), (1,)), ((), ())), preferred_element_type=F32)
    act = 0.5 * a * (1.0 + lax.erf(a * INV_SQRT2))
    rows = PEER_EB // PEER_KEYS
    gate = jnp.concatenate(
        [gs_ref[pl.ds(pl.multiple_of((e * rows + r) * TB, TB), TB), :] for r in range(rows)], axis=1)
    acc_ref[...] += jnp.dot((gate * act).astype(BF16), v_ref[...], preferred_element_type=F32)

    @pl.when(e == pl.num_programs(1) - 1)
    def _finish():
        o_ref[...] = x1_ref[...] + mod_ref[0][5:6] * acc_ref[...]


def _peer(h2, i1, i2, g, u, v, x1, mod, mod_row):
    T = h2.shape[0]
    TB = PEER_TB
    tok = lambda n: pl.BlockSpec((TB, n), lambda i, e: (i, 0))
    tab = pl.BlockSpec((PEER_EB, D_MODEL), lambda i, e: (e, 0))
    return pl.pallas_call(
        _peer_kernel,
        grid=(T // TB, PEER_EXPERTS // PEER_EB),
        in_specs=[tok(D_MODEL), tok(PEER_PICKS), tok(PEER_PICKS), tok(PEER_PICKS), tab, tab,
                  tok(D_MODEL),
                  pl.BlockSpec((1, 8, D_MODEL), lambda i, e: (mod_row(i), 0, 0))],
        out_specs=tok(D_MODEL),
        out_shape=jax.ShapeDtypeStruct((T, D_MODEL), F32),
        scratch_shapes=[pltpu.VMEM((PEER_KEYS * TB, PEER_KEYS), F32),
                        pltpu.VMEM((TB, D_MODEL), F32)],
        compiler_params=_cparams(("arbitrary", "arbitrary")),
        name="peer_experts",
    )(h2, i1, i2, g, u, v, x1, mod)


def _final_kernel(x_ref, g_ref, o_ref):
    x = x_ref[...]
    o_ref[...] = x * lax.rsqrt(jnp.mean(x * x, axis=-1, keepdims=True) + NORM_EPS) * g_ref[...]


def _final_norm(x, g):
    T = x.shape[0]
    return pl.pallas_call(
        _final_kernel,
        grid=(T // TM,),
        in_specs=[pl.BlockSpec((TM, D_MODEL), lambda i: (i, 0)), pl.BlockSpec(g.shape, lambda i: (0, 0))],
        out_specs=pl.BlockSpec((TM, D_MODEL), lambda i: (i, 0)),
        out_shape=jax.ShapeDtypeStruct((T, D_MODEL), F32),
        compiler_params=_cparams(("arbitrary",)),
        name="final_norm",
    )(x, g)


def _pad_heads(w, nh):
    lead = w.shape[:-1]
    w = w.reshape(lead + (nh, HEAD))
    w = jnp.pad(w, [(0, 0)] * len(lead) + [(0, 0), (0, HP - HEAD)])
    return w.reshape(lead + (nh * HP,))


def _pad_head_rows(w, nh):
    return _pad_heads(w.T, nh).T


def _rope_partner(w, nh):
    k = w.shape[0]
    w5 = w.reshape(k, nh, 2, 2, HEAD // 4)
    return jnp.concatenate([-w5[:, :, :, 1:2, :], w5[:, :, :, 0:1, :]], axis=3).reshape(k, nh * HEAD)


def _rope_tables(n_ctx, n_lat_batch, lat_seq):
    half = HEAD // 2
    inv = ROPE_THETA ** (-jnp.arange(0, half, 2, dtype=F32) / half)
    pos = jnp.arange(lat_seq)
    ang_r = (pos // GRID_W).astype(F32)[:, None] * inv[None, :]
    ang_c = (pos % GRID_W).astype(F32)[:, None] * inv[None, :]
    ang = jnp.concatenate([ang_r, ang_r, ang_c, ang_c], axis=1)
    pad = lambda a: jnp.pad(a, ((0, 0), (0, HP - HEAD)))
    cos_l, sin_l = pad(jnp.cos(ang)), pad(jnp.sin(ang))
    cos = jnp.concatenate([jnp.ones((n_ctx, HP), F32)] + [cos_l] * n_lat_batch, axis=0)
    sin = jnp.concatenate([jnp.zeros((n_ctx, HP), F32)] + [sin_l] * n_lat_batch, axis=0)
    return cos, sin


def _layer_weights(l, w_in, gate_b, attn_sink, mlstm_norm_g, pool_w, pool_scale, w_out, peer_wq, peer_keys):
    w = w_in[l]
    sizes = (ATTN_HEADS * HEAD, ATTN_KV_HEADS * HEAD, ATTN_KV_HEADS * HEAD) + (ML_HEADS * HEAD,) * 4 \
        + (N_GATES, POOL_GROUPS * HEAD)
    offs = np.cumsum((0,) + sizes)
    qa, ka, va, qm, km, vm, om, gm, xp = [w[:, offs[i]:offs[i + 1]] for i in range(len(sizes))]
    scale = HEAD ** -0.5
    cols = [_pad_heads(qa * scale, ATTN_HEADS), _pad_heads(ka, ATTN_KV_HEADS), _pad_heads(va, ATTN_KV_HEADS),
            _pad_heads(_rope_partner(qa, ATTN_HEADS) * scale, ATTN_HEADS),
            _pad_heads(_rope_partner(ka, ATTN_KV_HEADS), ATTN_KV_HEADS),
            _pad_heads(qm, ML_HEADS), _pad_heads(km * scale, ML_HEADS), _pad_heads(vm, ML_HEADS),
            _pad_heads(om, ML_HEADS), jnp.pad(gm, ((0, 0), (0, HP - N_GATES))), _pad_heads(xp, POOL_GROUPS)]
    w_proj = jnp.concatenate(cols, axis=1).astype(BF16)
    wo = w_out[l]
    n_a, n_m = ATTN_HEADS * HEAD, ML_HEADS * HEAD
    w_o = jnp.concatenate([_pad_head_rows(wo[:n_a], ATTN_HEADS),
                           _pad_head_rows(wo[n_a:n_a + n_m], ML_HEADS),
                           _pad_head_rows(wo[n_a + n_m:], POOL_GROUPS)], axis=0).astype(BF16)
    pw = jnp.pad(pool_w[l], ((0, 0), (0, HP - HEAD), (0, HP - HEAD))).astype(BF16)
    return dict(
        w_proj=w_proj, w_out=w_o, pool_w=pw,
        gate_b=jnp.pad(gate_b[l], (0, HP - N_GATES)).reshape(1, HP),
        sink=jnp.broadcast_to(attn_sink[l][:, None], (ATTN_HEADS, HP)),
        mnorm_g=_pad_heads(mlstm_norm_g[l], ML_HEADS).reshape(1, N_ML),
        pool_scale=_pad_heads(pool_scale[l], POOL_GROUPS).reshape(1, N_XP),
        wq_t=peer_wq[l].T.astype(BF16),
        keys=peer_keys[l].reshape(2 * PEER_HEADS, PEER_KEYS, PEER_HALF).astype(BF16),
    )


def _unpad_heads(a, nh):
    return a.reshape(a.shape[:-1] + (nh, HP))[..., :HEAD]


def kernel(x_prompt, x_sample, cache_k, cache_v, state_C, state_n, state_m, c, c_ctx, w_mod, b_mod, norm1_g, norm2_g, w_in, gate_b, attn_sink, mlstm_norm_g, pool_w, pool_scale, w_out, peer_wq, peer_keys, peer_u, peer_v, final_norm_g):
    B, S, _ = x_prompt.shape
    BL, L, _ = x_sample.shape
    n_ctx, n_lat = B * S, BL * L
    T = n_ctx + n_lat
    ctx_blocks = n_ctx // TM

    def mod_row_for(block):
        per_lat = L // block
        nctx = n_ctx // block
        return lambda i: jnp.where(i < nctx, 0, 1 + (i - nctx) // per_lat)

    cond = jnp.concatenate([c_ctx[None, :], c, jnp.zeros((8 - 1 - BL, D_MODEL), F32)], axis=0)
    mods = _modulation(cond, w_mod, b_mod)
    mods = mods[:, :1 + BL].reshape(DEPTH, 1 + BL, 6, D_MODEL)
    mods = jnp.pad(mods, ((0, 0), (0, 0), (0, 2), (0, 0)))

    cos, sin = _rope_tables(n_ctx, BL, L)
    x = jnp.concatenate([x_prompt.reshape(n_ctx, D_MODEL), x_sample.reshape(n_lat, D_MODEL)], axis=0)
    zc = jnp.zeros((B, ML_HEADS, HP, HP), F32)
    zv = jnp.zeros((B, ML_HEADS, 8, HP), F32)
    ks, vs, Cs, ns, ms = [], [], [], [], []
    for l in range(DEPTH):
        p = _layer_weights(l, w_in, gate_b, attn_sink, mlstm_norm_g, pool_w, pool_scale, w_out,
                           peer_wq, peer_keys)
        mod = mods[l]
        qa, ka, va, qm, km, vm, om, gm, xp = _pre(
            x, mod, mod_row_for(TM), norm1_g[l].reshape(1, D_MODEL), p["w_proj"], p["gate_b"], cos, sin)
        gmt = gm[:, :N_GATES].T

        kc = _pad_heads(cache_k[:, l].reshape(BL, -1, ATTN_KV_HEADS * HEAD), ATTN_KV_HEADS).astype(BF16)
        vc = _pad_heads(cache_v[:, l].reshape(BL, -1, ATTN_KV_HEADS * HEAD), ATTN_KV_HEADS).astype(BF16)
        attn = jnp.concatenate([
            _ctx_attention(qa, ka, va, p["sink"], B, S),
            _lat_attention(qa, ka, va, kc, vc, p["sink"], BL, L, n_ctx)], axis=0)

        hf, hb, fins = [], [], []
        for rev in (False, True):
            d = int(rev)
            h_c, c_c, n_c, m_c = _mlstm(qm, km, vm, gm, gmt, zc, zv, zv, B, S, 0, rev)
            c0 = jnp.pad(state_C[:, l, d], ((0, 0), (0, 0), (0, HP - HEAD), (0, HP - HEAD)))
            n0 = jnp.broadcast_to(jnp.pad(state_n[:, l, d], ((0, 0), (0, 0), (0, HP - HEAD)))[:, :, None, :],
                                  (BL, ML_HEADS, 8, HP))
            m0 = jnp.broadcast_to(state_m[:, l, d][:, :, None, None], (BL, ML_HEADS, 8, HP))
            h_l, _, _, _ = _mlstm(qm, km, vm, gm, gmt, c0, n0, m0, BL, L, n_ctx, rev)
            (hb if rev else hf).append(jnp.concatenate([h_c, h_l], axis=0))
            fins.append((c_c[:, :, :HEAD, :HEAD], n_c[:, :, 0, :HEAD], m_c[:, :, 0, 0]))
        Cs.append(jnp.stack([fins[0][0], fins[1][0]], axis=1))
        ns.append(jnp.stack([fins[0][1], fins[1][1]], axis=1))
        ms.append(jnp.stack([fins[0][2], fins[1][2]], axis=1))
        ks.append(_unpad_heads(ka[:n_ctx], ATTN_KV_HEADS).reshape(B, S, ATTN_KV_HEADS, HEAD))
        vs.append(_unpad_heads(va[:n_ctx], ATTN_KV_HEADS).reshape(B, S, ATTN_KV_HEADS, HEAD))

        pooled = jnp.concatenate([_pool(xp, p["pool_w"], p["pool_scale"], B, S, 0),
                                  _pool(xp, p["pool_w"], p["pool_scale"], BL, L, n_ctx)], axis=0)

        x1, h2, st = _post(x, attn, hf[0], hb[0], om, pooled, mod, mod_row_for(TM), p["mnorm_g"],
                           norm2_g[l].reshape(1, D_MODEL), p["w_out"], p["wq_t"], p["keys"])
        i1, i2, g = _peer_topk(st)
        x = _peer(h2, i1.T, i2.T, g.T, peer_u[l].astype(BF16), peer_v[l].astype(BF16), x1, mod,
                  mod_row_for(PEER_TB))

    y = _final_norm(x, final_norm_g.reshape(1, D_MODEL))
    return (y[:n_ctx].reshape(B, S, D_MODEL), y[n_ctx:].reshape(BL, L, D_MODEL),
            jnp.stack(ks, axis=1), jnp.stack(vs, axis=1),
            jnp.stack(Cs, axis=1), jnp.stack(ns, axis=1), jnp.stack(ms, axis=1))
```
